```python
import math
import jax, jax.numpy as jnp
from jax import lax
import numpy as np

D_MODEL = 1024
BATCH = 16
SEQ = 2048
DEPTH = 4

D_MIX = 2 * D_MODEL
ATTN_WIDTH = D_MIX // 2
V_HEAD_DIM = 128
QK_HEAD_DIM = V_HEAD_DIM // 2
ATTN_HEADS = ATTN_WIDTH // V_HEAD_DIM
Q_WIDTH = ATTN_HEADS * 2 * QK_HEAD_DIM
K_WIDTH = ATTN_HEADS * 2 * QK_HEAD_DIM
V_WIDTH = ATTN_HEADS * V_HEAD_DIM
LRU_WIDTH = D_MIX - ATTN_WIDTH
LRU_BLOCKS = 8
LRU_BLOCK = LRU_WIDTH // LRU_BLOCKS
LRU_C = 8.0
CONV_WIDTH = 4
Q_BLOCK = 128
NORM_EPS = 1e-6
IN_SPLITS = (Q_WIDTH, K_WIDTH, V_WIDTH, ATTN_WIDTH, LRU_WIDTH, LRU_WIDTH)
D_IN_PROJ = sum(IN_SPLITS)
SPLIT_POINTS = tuple(int(v) for v in np.cumsum(IN_SPLITS)[:-1])

kernel_name = "hymba_diffattn_rglru_sandwich"


def rmsnorm(x, g):
    xf = x.astype(jnp.float32)
    y = xf * lax.rsqrt(jnp.mean(xf * xf, axis=-1, keepdims=True) + NORM_EPS)
    return (y * g.astype(jnp.float32)).astype(x.dtype)


def lambda_init_for(layer_idx):
    return 0.8 - 0.6 * math.exp(-0.3 * layer_idx)


def diff_attention(q, k, v, lam):
    S = q.shape[1]
    scale = QK_HEAD_DIM ** -0.5
    outs = []
    for i in range(S // Q_BLOCK):
        q0 = i * Q_BLOCK
        end = q0 + Q_BLOCK
        s = jnp.einsum('bqhcd,bkhcd->bhcqk', q[:, q0:end], k[:, :end],
                       preferred_element_type=jnp.float32) * scale
        qpos = q0 + jnp.arange(Q_BLOCK)
        kpos = jnp.arange(end)
        causal = kpos[None, :] <= qpos[:, None]
        p = jax.nn.softmax(jnp.where(causal, s, -jnp.inf), axis=-1)
        w = p[:, :, 0] - lam * p[:, :, 1]
        outs.append(jnp.einsum('bhqk,bkhe->bqhe', w, v[:, :end].astype(jnp.float32)))
    return jnp.concatenate(outs, axis=1)


def causal_depthwise_conv(x, w, b):
    y = lax.conv_general_dilated(
        x, w[:, None, :].astype(x.dtype), window_strides=(1,), padding=[(CONV_WIDTH - 1, 0)],
        dimension_numbers=('NWC', 'WIO', 'NWC'), feature_group_count=x.shape[-1])
    return y + b.astype(x.dtype)


def _lin_combine(c1, c2):
    a1, b1 = c1
    a2, b2 = c2
    return a1 * a2, a2 * b1 + b2


def rg_lru(xc, w_r, b_r, w_i, b_i, lam_param):
    B, S, _ = xc.shape
    xb = xc.reshape(B, S, LRU_BLOCKS, LRU_BLOCK).astype(jnp.float32)
    r = jax.nn.sigmoid(jnp.einsum('bsnc,ncd->bsnd', xb, w_r.astype(jnp.float32)).reshape(B, S, LRU_WIDTH)
                       + b_r.astype(jnp.float32))
    gi = jax.nn.sigmoid(jnp.einsum('bsnc,ncd->bsnd', xb, w_i.astype(jnp.float32)).reshape(B, S, LRU_WIDTH)
                        + b_i.astype(jnp.float32))
    log_a = -LRU_C * r * jax.nn.softplus(-lam_param.astype(jnp.float32))
    a = jnp.exp(log_a)
    mult = jnp.sqrt(jnp.maximum(-jnp.expm1(2.0 * log_a), 0.0))
    bterm = mult * gi * xc.astype(jnp.float32)
    _, h = lax.associative_scan(_lin_combine, (a, bterm), axis=1)
    return h


def setup_inputs(seed: int = 0) -> dict:
    key = jax.random.key(seed)
    ks = jax.random.split(key, 20)
    f32 = jnp.float32
    x = jax.random.normal(ks[0], (BATCH, SEQ, D_MODEL), f32)
    pre_norm_g = 1.0 + 0.05 * jax.random.normal(ks[1], (DEPTH, D_MODEL), f32)
    w_in = jax.random.normal(ks[2], (DEPTH, D_MODEL, D_IN_PROJ), f32) * D_MODEL ** -0.5
    lambda_q1 = 0.1 * jax.random.normal(ks[3], (DEPTH, QK_HEAD_DIM), f32)
    lambda_k1 = 0.1 * jax.random.normal(ks[4], (DEPTH, QK_HEAD_DIM), f32)
    lambda_q2 = 0.1 * jax.random.normal(ks[5], (DEPTH, QK_HEAD_DIM), f32)
    lambda_k2 = 0.1 * jax.random.normal(ks[6], (DEPTH, QK_HEAD_DIM), f32)
    subln_g = 1.0 + 0.05 * jax.random.normal(ks[7], (DEPTH, V_HEAD_DIM), f32)
    conv_w = jax.random.normal(ks[8], (DEPTH, CONV_WIDTH, LRU_WIDTH), f32) * CONV_WIDTH ** -0.5
    conv_b = 0.02 * jax.random.normal(ks[9], (DEPTH, LRU_WIDTH), f32)
    w_rgate = jax.random.normal(ks[10], (DEPTH, LRU_BLOCKS, LRU_BLOCK, LRU_BLOCK), f32) * LRU_BLOCK ** -0.5
    b_rgate = 0.02 * jax.random.normal(ks[11], (DEPTH, LRU_WIDTH), f32)
    w_igate = jax.random.normal(ks[12], (DEPTH, LRU_BLOCKS, LRU_BLOCK, LRU_BLOCK), f32) * LRU_BLOCK ** -0.5
    b_igate = 0.02 * jax.random.normal(ks[13], (DEPTH, LRU_WIDTH), f32)
    a0 = jax.random.uniform(ks[14], (DEPTH, LRU_WIDTH), f32, 0.9, 0.999)
    s0 = a0 ** (1.0 / LRU_C)
    lru_lambda = jnp.log(s0) - jnp.log1p(-s0)
    w_out = jax.random.normal(ks[15], (DEPTH, D_MIX, D_MODEL), f32) * D_MIX ** -0.5
    post_norm_g = 1.0 + 0.05 * jax.random.normal(ks[16], (DEPTH, D_MODEL), f32)
    return {"x": x, "pre_norm_g": pre_norm_g, "w_in": w_in,
            "lambda_q1": lambda_q1, "lambda_k1": lambda_k1,
            "lambda_q2": lambda_q2, "lambda_k2": lambda_k2, "subln_g": subln_g,
            "conv_w": conv_w, "conv_b": conv_b,
            "w_rgate": w_rgate, "b_rgate": b_rgate, "w_igate": w_igate, "b_igate": b_igate,
            "lru_lambda": lru_lambda, "w_out": w_out, "post_norm_g": post_norm_g}


def reference(x, pre_norm_g, w_in, lambda_q1, lambda_k1, lambda_q2, lambda_k2, subln_g,
              conv_w, conv_b, w_rgate, b_rgate, w_igate, b_igate, lru_lambda, w_out, post_norm_g):
    B, S, _ = x.shape
    for l in range(DEPTH):
        h = rmsnorm(x, pre_norm_g[l])
        proj = jnp.einsum('bsd,de->bse', h, w_in[l])
        q, k, v, g_attn, x_lru, g_lru = jnp.split(proj, SPLIT_POINTS, axis=-1)

        lam_init = lambda_init_for(l)
        lam = (jnp.exp(jnp.sum(lambda_q1[l].astype(jnp.float32) * lambda_k1[l].astype(jnp.float32)))
               - jnp.exp(jnp.sum(lambda_q2[l].astype(jnp.float32) * lambda_k2[l].astype(jnp.float32)))
               + lam_init)
        qh = q.reshape(B, S, ATTN_HEADS, 2, QK_HEAD_DIM)
        kh = k.reshape(B, S, ATTN_HEADS, 2, QK_HEAD_DIM)
        vh = v.reshape(B, S, ATTN_HEADS, V_HEAD_DIM)
        o = diff_attention(qh, kh, vh, lam)
        o = rmsnorm(o, subln_g[l]) * (1.0 - lam_init)
        y_attn = o.reshape(B, S, ATTN_WIDTH).astype(x.dtype) * jax.nn.silu(g_attn)

        xc = causal_depthwise_conv(x_lru, conv_w[l], conv_b[l])
        hr = rg_lru(xc, w_rgate[l], b_rgate[l], w_igate[l], b_igate[l], lru_lambda[l])
        y_lru = hr.astype(x.dtype) * jax.nn.silu(g_lru)

        y = jnp.concatenate([y_attn, y_lru], axis=-1)
        y = jnp.einsum('bse,ed->bsd', y, w_out[l])
        x = x + rmsnorm(y, post_norm_g[l])
    return x
```

```python
import functools
import math

import jax
import jax.numpy as jnp
from jax import lax
from jax.experimental import pallas as pl
from jax.experimental.pallas import tpu as pltpu

F32 = jnp.float32
BF16 = jnp.bfloat16

D_MODEL = 1024
DEPTH = 4
ATTN_WIDTH = 1024
V_HEAD_DIM = 128
QK_HEAD_DIM = 64
ATTN_HEADS = 8
LRU_WIDTH = 1024
LRU_BLOCKS = 8
LRU_BLOCK = 128
LRU_C = 8.0
CONV_WIDTH = 4
NORM_EPS = 1e-6
N_PROJ = 6

Q_PRESCALE = QK_HEAD_DIM ** -0.5 * math.log2(math.e)

V7X_VMEM_BYTES = 64 * 1024 * 1024
VMEM_LIMIT_BYTES = 56 * 1024 * 1024

ROW_TILE = 512
Q_TILE = 256


def _silu(x):
    return x * jax.nn.sigmoid(x)


def _rms_scale(x):
    return x * lax.rsqrt(jnp.mean(x * x, axis=-1, keepdims=True) + NORM_EPS)


def _inproj_kernel(x_ref, g_ref, w_ref, q_ref, k_ref, v_ref, ga_ref, xl_ref, gl_ref):
    h = (_rms_scale(x_ref[...]) * g_ref[...]).astype(BF16)

    def proj(j):
        return jnp.dot(h, w_ref[:, j * D_MODEL:(j + 1) * D_MODEL], preferred_element_type=F32)

    q_ref[...] = (proj(0) * Q_PRESCALE).astype(BF16)
    k_ref[...] = proj(1).astype(BF16)
    v_ref[...] = proj(2).astype(BF16)
    ga_ref[...] = _silu(proj(3)).astype(BF16)
    xl_ref[...] = proj(4).astype(BF16)
    gl_ref[...] = _silu(proj(5)).astype(BF16)


def _inproj(x, g, w):
    B, S, D = x.shape
    tm = ROW_TILE
    bs_spec = pl.BlockSpec((None, tm, D), lambda b, i: (b, i, 0))
    tb_spec = pl.BlockSpec((tm, D), lambda b, i: (i, b))
    bs_shape = jax.ShapeDtypeStruct((B, S, D), BF16)
    tb_shape = jax.ShapeDtypeStruct((S, B * D), BF16)
    return pl.pallas_call(
        _inproj_kernel,
        grid=(B, S // tm),
        in_specs=[bs_spec,
                  pl.BlockSpec((1, D), lambda b, i: (0, 0)),
                  pl.BlockSpec((D, N_PROJ * D), lambda b, i: (0, 0))],
        out_specs=[bs_spec, bs_spec, bs_spec, bs_spec, tb_spec, tb_spec],
        out_shape=[bs_shape, bs_shape, bs_shape, bs_shape, tb_shape, tb_shape],
        compiler_params=pltpu.CompilerParams(
            dimension_semantics=("arbitrary", "arbitrary"), vmem_limit_bytes=VMEM_LIMIT_BYTES),
        name="inproj",
    )(x, g, w)


def _attn_kernel(lq1_ref, lk1_ref, lq2_ref, lk2_ref, sg_ref, q_ref, k_ref, v_ref, ga_ref, o_ref,
                 *, lam_init):
    S = q_ref.shape[0]
    tq = Q_TILE
    lam = (jnp.exp(jnp.sum(lq1_ref[...] * lk1_ref[...], axis=-1, keepdims=True))
           - jnp.exp(jnp.sum(lq2_ref[...] * lk2_ref[...], axis=-1, keepdims=True))
           + lam_init)
    lane = lax.broadcasted_iota(jnp.int32, (tq, 2 * QK_HEAD_DIM), 1)
    first_map = lane < QK_HEAD_DIM
    row = lax.broadcasted_iota(jnp.int32, (2 * tq, tq), 0)
    col = lax.broadcasted_iota(jnp.int32, (2 * tq, tq), 1)
    causal = col <= jnp.where(row >= tq, row - tq, row)
    out_gain = sg_ref[...] * (1.0 - lam_init)

    for i in range(S // tq):
        r0, tk = i * tq, (i + 1) * tq
        qt = q_ref[r0:r0 + tq, :]
        zero = jnp.zeros_like(qt)
        qs = jnp.concatenate([jnp.where(first_map, qt, zero), jnp.where(first_map, zero, qt)], axis=0)
        s = lax.dot_general(qs, k_ref[0:tk, :], (((1,), (1,)), ((), ())),
                            preferred_element_type=F32)
        diag = jnp.where(causal, s[:, r0:tk], -jnp.inf)
        s = diag if i == 0 else jnp.concatenate([s[:, :r0], diag], axis=1)
        m = jnp.max(s, axis=-1, keepdims=True)
        p = jnp.exp2(s - m)
        inv = 1.0 / jnp.sum(p, axis=-1, keepdims=True)
        w = (p[:tq] * inv[:tq] - p[tq:] * (lam * inv[tq:])).astype(BF16)
        o = jnp.dot(w, v_ref[0:tk, :], preferred_element_type=F32)
        o = _rms_scale(o) * out_gain
        o_ref[r0:r0 + tq, :] = (o * ga_ref[r0:r0 + tq, :].astype(F32)).astype(BF16)


def _attention(q, k, v, ga, lq1, lk1, lq2, lk2, sg, lam_init):
    B, S, _ = q.shape
    head_spec = pl.BlockSpec((None, S, V_HEAD_DIM), lambda b, h: (b, 0, h))
    lam_spec = pl.BlockSpec((1, QK_HEAD_DIM), lambda b, h: (0, 0))
    return pl.pallas_call(
        functools.partial(_attn_kernel, lam_init=lam_init),
        grid=(B, ATTN_HEADS),
        in_specs=[lam_spec, lam_spec, lam_spec, lam_spec,
                  pl.BlockSpec((1, V_HEAD_DIM), lambda b, h: (0, 0)),
                  head_spec, head_spec, head_spec, head_spec],
        out_specs=head_spec,
        out_shape=jax.ShapeDtypeStruct((B, S, ATTN_WIDTH), BF16),
        compiler_params=pltpu.CompilerParams(
            dimension_semantics=("arbitrary", "arbitrary"), vmem_limit_bytes=VMEM_LIMIT_BYTES),
        name="diff_attn",
    )(lq1, lk1, lq2, lk2, sg, q, k, v, ga)


def _lru_kernel(xl_ref, gl_ref, cw_ref, cb_ref, wg_ref, br_ref, bi_ref, lam_ref, y_ref,
                xprev_ref, hprev_ref, a_ref, b_ref, *, batch):
    R = xl_ref.shape[0]
    halo = (CONV_WIDTH - 1) * batch

    @pl.when(pl.program_id(0) == 0)
    def _():
        xprev_ref[...] = jnp.zeros_like(xprev_ref)
        hprev_ref[...] = jnp.zeros_like(hprev_ref)

    x = xl_ref[...].astype(F32)
    xe = jnp.concatenate([xprev_ref[...], x], axis=0)
    xprev_ref[...] = x[R - halo:, :]
    xc = cb_ref[...] + sum(cw_ref[j:j + 1, :] * xe[j * batch:j * batch + R, :]
                           for j in range(CONV_WIDTH))

    pre = [jnp.dot(xc[:, n * LRU_BLOCK:(n + 1) * LRU_BLOCK].astype(BF16), wg_ref[n],
                   preferred_element_type=F32) for n in range(LRU_BLOCKS)]
    r = jax.nn.sigmoid(jnp.concatenate([g[:, :LRU_BLOCK] for g in pre], axis=1) + br_ref[...])
    gi = jax.nn.sigmoid(jnp.concatenate([g[:, LRU_BLOCK:] for g in pre], axis=1) + bi_ref[...])

    z = -lam_ref[...]
    softplus = jnp.maximum(z, 0.0) + jnp.log1p(jnp.exp(-jnp.abs(z)))
    log_a = (-LRU_C * softplus) * r
    a = jnp.exp(log_a)
    mult = jnp.sqrt(jnp.maximum(-jnp.tanh(log_a) * (a * a + 1.0), 0.0))
    a_ref[...] = a
    b_ref[...] = mult * gi * xc

    h = hprev_ref[...]
    for t in range(R // batch):
        rows = slice(t * batch, (t + 1) * batch)
        h = a_ref[rows, :] * h + b_ref[rows, :]
        a_ref[rows, :] = h
    hprev_ref[...] = h
    y_ref[...] = (a_ref[...] * gl_ref[...].astype(F32)).astype(BF16)


def _lru(xl, gl, cw, cb, wg, br, bi, lam, batch):
    N, W = xl.shape
    R = ROW_TILE
    row_spec = pl.BlockSpec((R, W), lambda i: (i, 0))
    vec_spec = pl.BlockSpec((1, W), lambda i: (0, 0))
    return pl.pallas_call(
        functools.partial(_lru_kernel, batch=batch),
        grid=(N // R,),
        in_specs=[row_spec, row_spec,
                  pl.BlockSpec((CONV_WIDTH, W), lambda i: (0, 0)), vec_spec,
                  pl.BlockSpec((LRU_BLOCKS, LRU_BLOCK, 2 * LRU_BLOCK), lambda i: (0, 0, 0)),
                  vec_spec, vec_spec, vec_spec],
        out_specs=row_spec,
        out_shape=jax.ShapeDtypeStruct((N, W), BF16),
        scratch_shapes=[pltpu.VMEM(((CONV_WIDTH - 1) * batch, W), F32),
                        pltpu.VMEM((batch, W), F32),
                        pltpu.VMEM((R, W), F32),
                        pltpu.VMEM((R, W), F32)],
        compiler_params=pltpu.CompilerParams(
            dimension_semantics=("arbitrary",), vmem_limit_bytes=VMEM_LIMIT_BYTES),
        name="rg_lru",
    )(xl, gl, cw, cb, wg, br, bi, lam)


def _outproj_kernel(ya_ref, yl_ref, w_ref, x_ref, g_ref, o_ref):
    acc = jnp.dot(ya_ref[...], w_ref[:ATTN_WIDTH, :], preferred_element_type=F32)
    acc = acc + jnp.dot(yl_ref[...], w_ref[ATTN_WIDTH:, :], preferred_element_type=F32)
    o_ref[...] = x_ref[...] + _rms_scale(acc) * g_ref[...]


def _outproj(ya, yl, w, x, g):
    B, S, D = x.shape
    tm = ROW_TILE
    bs_spec = pl.BlockSpec((None, tm, D), lambda b, i: (b, i, 0))
    return pl.pallas_call(
        _outproj_kernel,
        grid=(B, S // tm),
        in_specs=[bs_spec,
                  pl.BlockSpec((tm, LRU_WIDTH), lambda b, i: (i, b)),
                  pl.BlockSpec((ATTN_WIDTH + LRU_WIDTH, D), lambda b, i: (0, 0)),
                  bs_spec,
                  pl.BlockSpec((1, D), lambda b, i: (0, 0))],
        out_specs=bs_spec,
        out_shape=jax.ShapeDtypeStruct((B, S, D), F32),
        compiler_params=pltpu.CompilerParams(
            dimension_semantics=("arbitrary", "arbitrary"), vmem_limit_bytes=VMEM_LIMIT_BYTES),
        name="outproj",
    )(ya, yl, w, x, g)


def kernel(x, pre_norm_g, w_in, lambda_q1, lambda_k1, lambda_q2, lambda_k2, subln_g, conv_w, conv_b,
           w_rgate, b_rgate, w_igate, b_igate, lru_lambda, w_out, post_norm_g):
    B, S, D = x.shape
    assert D == D_MODEL and S % ROW_TILE == 0 and S % Q_TILE == 0
    assert ROW_TILE % B == 0 and (CONV_WIDTH - 1) * B <= ROW_TILE
    w_in_bf = w_in.astype(BF16)
    w_out_bf = w_out.astype(BF16)
    w_gate_bf = jnp.concatenate([w_rgate, w_igate], axis=-1).astype(BF16)
    for l in range(DEPTH):
        lam_init = 0.8 - 0.6 * math.exp(-0.3 * l)
        q, k, v, ga, xl, gl = _inproj(x, pre_norm_g[l][None], w_in_bf[l])
        ya = _attention(q, k, v, ga, lambda_q1[l][None], lambda_k1[l][None], lambda_q2[l][None],
                        lambda_k2[l][None], subln_g[l][None], lam_init)
        yl = _lru(xl.reshape(S * B, LRU_WIDTH), gl.reshape(S * B, LRU_WIDTH), conv_w[l],
                  conv_b[l][None], w_gate_bf[l], b_rgate[l][None], b_igate[l][None],
                  lru_lambda[l][None], B)
        x = _outproj(ya, yl.reshape(S, B * LRU_WIDTH), w_out_bf[l], x, post_norm_g[l][None])
    return x
```

```python
import functools
import math

import jax
import jax.numpy as jnp
import numpy as np
from jax import lax
from jax.experimental import pallas as pl
from jax.experimental.pallas import tpu as pltpu

F32 = jnp.float32
BF16 = jnp.bfloat16

D_MODEL = 1024
DEPTH = 4
ATTN_WIDTH = 1024
V_HEAD_DIM = 128
QK_HEAD_DIM = 64
ATTN_HEADS = 8
LRU_WIDTH = 1024
LRU_BLOCKS = 8
LRU_BLOCK = 128
LRU_C = 8.0
CONV_WIDTH = 4
NORM_EPS = 1e-6
N_PROJ = 6

Q_PRESCALE = QK_HEAD_DIM ** -0.5 * math.log2(math.e)

V7X_VMEM_BYTES = 64 * 1024 * 1024
VMEM_LIMIT_BYTES = 56 * 1024 * 1024

ROW_TILE = 512
Q_TILE = 256


def _silu(x):
    return x * jax.nn.sigmoid(x)


def _rms_scale(x):
    return x * lax.rsqrt(jnp.mean(x * x, axis=-1, keepdims=True) + NORM_EPS)


def _to_heads(ref, val):
    B, H, ts, hd = ref.shape
    for h in range(H):
        ref[:, h, :, :] = val[:, h * hd:(h + 1) * hd].reshape(B, ts, hd)


def _inproj_kernel(x_ref, g_ref, w_ref, perm_ref, q_ref, k_ref, v_ref, ga_ref, xl_ref, gl_ref):
    B, ts, D = x_ref.shape
    x = x_ref[...].reshape(B * ts, D)
    h = (_rms_scale(x) * g_ref[...]).astype(BF16)

    def proj(lhs, j):
        return jnp.dot(lhs, w_ref[:, j * D_MODEL:(j + 1) * D_MODEL], preferred_element_type=F32)

    _to_heads(q_ref, (proj(h, 0) * Q_PRESCALE).astype(BF16))
    _to_heads(k_ref, proj(h, 1).astype(BF16))
    _to_heads(v_ref, proj(h, 2).astype(BF16))
    _to_heads(ga_ref, _silu(proj(h, 3)).astype(BF16))
    h_tm = jnp.dot(perm_ref[...], h, preferred_element_type=F32).astype(BF16)
    xl_ref[...] = proj(h_tm, 4).astype(BF16)
    gl_ref[...] = _silu(proj(h_tm, 5)).astype(BF16)


def _inproj(x, g, w, perm):
    B, S, D = x.shape
    ts = ROW_TILE // B
    x_spec = pl.BlockSpec((B, ts, D), lambda i: (0, i, 0))
    head_spec = pl.BlockSpec((B, ATTN_HEADS, ts, V_HEAD_DIM), lambda i: (0, 0, i, 0))
    tm_spec = pl.BlockSpec((ROW_TILE, D), lambda i: (i, 0))
    head_shape = jax.ShapeDtypeStruct((B, ATTN_HEADS, S, V_HEAD_DIM), BF16)
    tm_shape = jax.ShapeDtypeStruct((S * B, D), BF16)
    return pl.pallas_call(
        _inproj_kernel,
        grid=(S // ts,),
        in_specs=[x_spec,
                  pl.BlockSpec((1, D), lambda i: (0, 0)),
                  pl.BlockSpec((D, N_PROJ * D), lambda i: (0, 0)),
                  pl.BlockSpec((ROW_TILE, ROW_TILE), lambda i: (0, 0))],
        out_specs=[head_spec, head_spec, head_spec, head_spec, tm_spec, tm_spec],
        out_shape=[head_shape, head_shape, head_shape, head_shape, tm_shape, tm_shape],
        compiler_params=pltpu.CompilerParams(
            dimension_semantics=("arbitrary",), vmem_limit_bytes=VMEM_LIMIT_BYTES),
        name="inproj",
    )(x, g, w, perm)


def _attn_kernel(lq1_ref, lk1_ref, lq2_ref, lk2_ref, sg_ref, q_ref, k_ref, v_ref, ga_ref, o_ref,
                 *, lam_init):
    S = q_ref.shape[0]
    tq = Q_TILE
    lam = (jnp.exp(jnp.sum(lq1_ref[...] * lk1_ref[...], axis=-1, keepdims=True))
           - jnp.exp(jnp.sum(lq2_ref[...] * lk2_ref[...], axis=-1, keepdims=True))
           + lam_init)
    lane = lax.broadcasted_iota(jnp.int32, (tq, 2 * QK_HEAD_DIM), 1)
    first_map = lane < QK_HEAD_DIM
    row = lax.broadcasted_iota(jnp.int32, (2 * tq, tq), 0)
    col = lax.broadcasted_iota(jnp.int32, (2 * tq, tq), 1)
    causal = col <= jnp.where(row >= tq, row - tq, row)
    out_gain = sg_ref[...] * (1.0 - lam_init)

    for i in range(S // tq):
        r0, tk = i * tq, (i + 1) * tq
        qt = q_ref[r0:r0 + tq, :]
        zero = jnp.zeros_like(qt)
        qs = jnp.concatenate([jnp.where(first_map, qt, zero), jnp.where(first_map, zero, qt)], axis=0)
        s = lax.dot_general(qs, k_ref[0:tk, :], (((1,), (1,)), ((), ())),
                            preferred_element_type=F32)
        diag = jnp.where(causal, s[:, r0:tk], -jnp.inf)
        s = diag if i == 0 else jnp.concatenate([s[:, :r0], diag], axis=1)
        m = jnp.max(s, axis=-1, keepdims=True)
        p = jnp.exp2(s - m)
        inv = 1.0 / jnp.sum(p, axis=-1, keepdims=True)
        w = (p[:tq] * inv[:tq] - p[tq:] * (lam * inv[tq:])).astype(BF16)
        o = jnp.dot(w, v_ref[0:tk, :], preferred_element_type=F32)
        o = _rms_scale(o) * out_gain
        o_ref[r0:r0 + tq, :] = (o * ga_ref[r0:r0 + tq, :].astype(F32)).astype(BF16)


def _attention(q, k, v, ga, lq1, lk1, lq2, lk2, sg, lam_init):
    B, _, S, _ = q.shape
    head_spec = pl.BlockSpec((None, None, S, V_HEAD_DIM), lambda b, h: (b, h, 0, 0))
    lam_spec = pl.BlockSpec((1, QK_HEAD_DIM), lambda b, h: (0, 0))
    return pl.pallas_call(
        functools.partial(_attn_kernel, lam_init=lam_init),
        grid=(B, ATTN_HEADS),
        in_specs=[lam_spec, lam_spec, lam_spec, lam_spec,
                  pl.BlockSpec((1, V_HEAD_DIM), lambda b, h: (0, 0)),
                  head_spec, head_spec, head_spec, head_spec],
        out_specs=head_spec,
        out_shape=jax.ShapeDtypeStruct((B, ATTN_HEADS, S, V_HEAD_DIM), BF16),
        compiler_params=pltpu.CompilerParams(
            dimension_semantics=("arbitrary", "arbitrary"), vmem_limit_bytes=VMEM_LIMIT_BYTES),
        name="diff_attn",
    )(lq1, lk1, lq2, lk2, sg, q, k, v, ga)


def _lru_kernel(xl_ref, gl_ref, cw_ref, cb_ref, wg_ref, br_ref, bi_ref, lam_ref, y_ref,
                xprev_ref, hprev_ref, a_ref, b_ref, *, batch):
    R = xl_ref.shape[0]
    halo = (CONV_WIDTH - 1) * batch

    @pl.when(pl.program_id(0) == 0)
    def _():
        xprev_ref[...] = jnp.zeros_like(xprev_ref)
        hprev_ref[...] = jnp.zeros_like(hprev_ref)

    x = xl_ref[...].astype(F32)
    xe = jnp.concatenate([xprev_ref[...], x], axis=0)
    xprev_ref[...] = x[R - halo:, :]
    xc = cb_ref[...] + sum(cw_ref[j:j + 1, :] * xe[j * batch:j * batch + R, :]
                           for j in range(CONV_WIDTH))

    pre = [jnp.dot(xc[:, n * LRU_BLOCK:(n + 1) * LRU_BLOCK].astype(BF16), wg_ref[n],
                   preferred_element_type=F32) for n in range(LRU_BLOCKS)]
    r = jax.nn.sigmoid(jnp.concatenate([g[:, :LRU_BLOCK] for g in pre], axis=1) + br_ref[...])
    gi = jax.nn.sigmoid(jnp.concatenate([g[:, LRU_BLOCK:] for g in pre], axis=1) + bi_ref[...])

    z = -lam_ref[...]
    softplus = jnp.maximum(z, 0.0) + jnp.log1p(jnp.exp(-jnp.abs(z)))
    log_a = (-LRU_C * softplus) * r
    a = jnp.exp(log_a)
    mult = jnp.sqrt(jnp.maximum(-jnp.tanh(log_a) * (a * a + 1.0), 0.0))
    a_ref[...] = a
    b_ref[...] = mult * gi * xc

    h = hprev_ref[...]
    for t in range(R // batch):
        rows = slice(t * batch, (t + 1) * batch)
        h = a_ref[rows, :] * h + b_ref[rows, :]
        a_ref[rows, :] = h
    hprev_ref[...] = h
    y_ref[...] = (a_ref[...] * gl_ref[...].astype(F32)).astype(BF16)


def _lru(xl, gl, cw, cb, wg, br, bi, lam, batch):
    N, W = xl.shape
    R = ROW_TILE
    row_spec = pl.BlockSpec((R, W), lambda i: (i, 0))
    vec_spec = pl.BlockSpec((1, W), lambda i: (0, 0))
    return pl.pallas_call(
        functools.partial(_lru_kernel, batch=batch),
        grid=(N // R,),
        in_specs=[row_spec, row_spec,
                  pl.BlockSpec((CONV_WIDTH, W), lambda i: (0, 0)), vec_spec,
                  pl.BlockSpec((LRU_BLOCKS, LRU_BLOCK, 2 * LRU_BLOCK), lambda i: (0, 0, 0)),
                  vec_spec, vec_spec, vec_spec],
        out_specs=row_spec,
        out_shape=jax.ShapeDtypeStruct((N, W), BF16),
        scratch_shapes=[pltpu.VMEM(((CONV_WIDTH - 1) * batch, W), F32),
                        pltpu.VMEM((batch, W), F32),
                        pltpu.VMEM((R, W), F32),
                        pltpu.VMEM((R, W), F32)],
        compiler_params=pltpu.CompilerParams(
            dimension_semantics=("arbitrary",), vmem_limit_bytes=VMEM_LIMIT_BYTES),
        name="rg_lru",
    )(xl, gl, cw, cb, wg, br, bi, lam)


def _outproj_kernel(ya_ref, yl_ref, w_ref, perm_t_ref, x_ref, g_ref, o_ref):
    B, H, ts, hd = ya_ref.shape
    ya = jnp.concatenate([ya_ref[:, h, :, :].reshape(B * ts, hd) for h in range(H)], axis=1)
    yl = jnp.dot(perm_t_ref[...], yl_ref[...], preferred_element_type=F32).astype(BF16)
    acc = jnp.dot(ya, w_ref[:ATTN_WIDTH, :], preferred_element_type=F32)
    acc = acc + jnp.dot(yl, w_ref[ATTN_WIDTH:, :], preferred_element_type=F32)
    y = _rms_scale(acc) * g_ref[...]
    o_ref[...] = x_ref[...] + y.reshape(B, ts, y.shape[-1])


def _outproj(ya, yl, w, perm_t, x, g):
    B, S, D = x.shape
    ts = ROW_TILE // B
    x_spec = pl.BlockSpec((B, ts, D), lambda i: (0, i, 0))
    return pl.pallas_call(
        _outproj_kernel,
        grid=(S // ts,),
        in_specs=[pl.BlockSpec((B, ATTN_HEADS, ts, V_HEAD_DIM), lambda i: (0, 0, i, 0)),
                  pl.BlockSpec((ROW_TILE, LRU_WIDTH), lambda i: (i, 0)),
                  pl.BlockSpec((ATTN_WIDTH + LRU_WIDTH, D), lambda i: (0, 0)),
                  pl.BlockSpec((ROW_TILE, ROW_TILE), lambda i: (0, 0)),
                  x_spec,
                  pl.BlockSpec((1, D), lambda i: (0, 0))],
        out_specs=x_spec,
        out_shape=jax.ShapeDtypeStruct((B, S, D), F32),
        compiler_params=pltpu.CompilerParams(
            dimension_semantics=("arbitrary",), vmem_limit_bytes=VMEM_LIMIT_BYTES),
        name="outproj",
    )(ya, yl, w, perm_t, x, g)


def kernel(x, pre_norm_g, w_in, lambda_q1, lambda_k1, lambda_q2, lambda_k2, subln_g, conv_w, conv_b,
           w_rgate, b_rgate, w_igate, b_igate, lru_lambda, w_out, post_norm_g):
    B, S, D = x.shape
    assert D == D_MODEL and S % ROW_TILE == 0 and S % Q_TILE == 0
    assert ROW_TILE % B == 0 and (CONV_WIDTH - 1) * B <= ROW_TILE
    w_in_bf = w_in.astype(BF16)
    w_out_bf = w_out.astype(BF16)
    w_gate_bf = jnp.concatenate([w_rgate, w_igate], axis=-1).astype(BF16)
    ts = ROW_TILE // B
    r = np.arange(ROW_TILE)
    perm_np = np.zeros((ROW_TILE, ROW_TILE), np.float32)
    perm_np[r, (r % B) * ts + r // B] = 1.0
    perm = jnp.asarray(perm_np, BF16)
    perm_t = jnp.asarray(perm_np.T, BF16)
    for l in range(DEPTH):
        lam_init = 0.8 - 0.6 * math.exp(-0.3 * l)
        q, k, v, ga, xl, gl = _inproj(x, pre_norm_g[l][None], w_in_bf[l], perm)
        ya = _attention(q, k, v, ga, lambda_q1[l][None], lambda_k1[l][None], lambda_q2[l][None],
                        lambda_k2[l][None], subln_g[l][None], lam_init)
        yl = _lru(xl, gl, conv_w[l], conv_b[l][None], w_gate_bf[l], b_rgate[l][None],
                  b_igate[l][None], lru_lambda[l][None], B)
        x = _outproj(ya, yl, w_out_bf[l], perm_t, x, post_norm_g[l][None])
    return x
```

```python
import functools
import math

import jax
import jax.numpy as jnp
import numpy as np
from jax import lax
from jax.experimental import pallas as pl
from jax.experimental.pallas import tpu as pltpu

F32 = jnp.float32
BF16 = jnp.bfloat16

D_MODEL = 1024
DEPTH = 4
ATTN_WIDTH = 1024
V_HEAD_DIM = 128
QK_HEAD_DIM = 64
ATTN_HEADS = 8
LRU_WIDTH = 1024
LRU_BLOCKS = 8
LRU_BLOCK = 128
LRU_C = 8.0
CONV_WIDTH = 4
NORM_EPS = 1e-6
N_PROJ = 6

Q_PRESCALE = QK_HEAD_DIM ** -0.5 * math.log2(math.e)

V7X_VMEM_BYTES = 64 * 1024 * 1024
VMEM_LIMIT_BYTES = 56 * 1024 * 1024

ROW_TILE = 512
Q_TILE = 256
LRU_ROWS = 1024
SCORE_CHUNK = 512
VALUE_CHUNK = 256


def _silu(x):
    return x * jax.nn.sigmoid(x)


def _rms_scale(x):
    return x * lax.rsqrt(jnp.mean(x * x, axis=-1, keepdims=True) + NORM_EPS)


def _to_heads(ref, val):
    B, H, ts, hd = ref.shape
    for h in range(H):
        ref[:, h, :, :] = val[:, h * hd:(h + 1) * hd].reshape(B, ts, hd)


def _inproj_kernel(x_ref, g_ref, w_ref, perm_ref, q_ref, k_ref, v_ref, ga_ref, xl_ref, gl_ref):
    B, ts, D = x_ref.shape
    x = x_ref[...].reshape(B * ts, D)
    h = (_rms_scale(x) * g_ref[...]).astype(BF16)

    def proj(lhs, j):
        return jnp.dot(lhs, w_ref[:, j * D_MODEL:(j + 1) * D_MODEL], preferred_element_type=F32)

    _to_heads(q_ref, (proj(h, 0) * Q_PRESCALE).astype(BF16))
    _to_heads(k_ref, proj(h, 1).astype(BF16))
    _to_heads(v_ref, proj(h, 2).astype(BF16))
    _to_heads(ga_ref, _silu(proj(h, 3)).astype(BF16))
    h_tm = jnp.dot(perm_ref[...], h, preferred_element_type=F32).astype(BF16)
    xl_ref[...] = proj(h_tm, 4).astype(BF16)
    gl_ref[...] = _silu(proj(h_tm, 5)).astype(BF16)


def _inproj(x, g, w, perm):
    B, S, D = x.shape
    ts = ROW_TILE // B
    x_spec = pl.BlockSpec((B, ts, D), lambda i: (0, i, 0))
    head_spec = pl.BlockSpec((B, ATTN_HEADS, ts, V_HEAD_DIM), lambda i: (0, 0, i, 0))
    tm_spec = pl.BlockSpec((ROW_TILE, D), lambda i: (i, 0))
    head_shape = jax.ShapeDtypeStruct((B, ATTN_HEADS, S, V_HEAD_DIM), BF16)
    tm_shape = jax.ShapeDtypeStruct((S * B, D), BF16)
    return pl.pallas_call(
        _inproj_kernel,
        grid=(S // ts,),
        in_specs=[x_spec,
                  pl.BlockSpec((1, D), lambda i: (0, 0)),
                  pl.BlockSpec((D, N_PROJ * D), lambda i: (0, 0)),
                  pl.BlockSpec((ROW_TILE, ROW_TILE), lambda i: (0, 0))],
        out_specs=[head_spec, head_spec, head_spec, head_spec, tm_spec, tm_spec],
        out_shape=[head_shape, head_shape, head_shape, head_shape, tm_shape, tm_shape],
        compiler_params=pltpu.CompilerParams(
            dimension_semantics=("arbitrary",), vmem_limit_bytes=VMEM_LIMIT_BYTES),
        name="inproj",
    )(x, g, w, perm)


def _rows8_reduce(x, op):
    acc = x[0:8, :]
    for j in range(1, x.shape[0] // 8):
        acc = op(acc, x[8 * j:8 * j + 8, :])
    return acc


def _attn_kernel(lq1_ref, lk1_ref, lq2_ref, lk2_ref, sg_ref, q_ref, k_ref, v_ref, ga_ref, o_ref,
                 *, lam_init):
    S = q_ref.shape[0]
    tq = Q_TILE
    n_tiles = S // tq
    lam = (jnp.exp(jnp.sum(lq1_ref[...] * lk1_ref[...], axis=-1, keepdims=True))
           - jnp.exp(jnp.sum(lq2_ref[...] * lk2_ref[...], axis=-1, keepdims=True))
           + lam_init)
    lane = lax.broadcasted_iota(jnp.int32, (tq, 2 * QK_HEAD_DIM), 1)
    first_map = lane < QK_HEAD_DIM
    krow = lax.broadcasted_iota(jnp.int32, (tq, 2 * tq), 0)
    qcol = lax.broadcasted_iota(jnp.int32, (tq, 2 * tq), 1)
    causal = krow <= jnp.where(qcol >= tq, qcol - tq, qcol)
    out_gain = sg_ref[...] * (1.0 - lam_init)
    vt = v_ref[...].T

    def scores(i):
        tk = (i + 1) * tq
        qt = q_ref[i * tq:tk, :]
        zero = jnp.zeros_like(qt)
        qs = jnp.concatenate([jnp.where(first_map, qt, zero), jnp.where(first_map, zero, qt)], axis=0)
        chunks, m8 = [], None
        for k0 in range(0, tk, SCORE_CHUNK):
            k1 = min(k0 + SCORE_CHUNK, tk)
            s = lax.dot_general(k_ref[k0:k1, :], qs, (((1,), (1,)), ((), ())),
                                preferred_element_type=F32)
            if k1 == tk:
                diag = jnp.where(causal, s[k1 - k0 - tq:, :], -jnp.inf)
                s = diag if k1 - k0 == tq else jnp.concatenate([s[:k1 - k0 - tq, :], diag], axis=0)
            chunks.append(s)
            cm = _rows8_reduce(s, jnp.maximum)
            m8 = cm if m8 is None else jnp.maximum(m8, cm)
        return (chunks[0] if len(chunks) == 1 else jnp.concatenate(chunks, axis=0)), m8

    def values(i, s, m8):
        tk = (i + 1) * tq
        m = jnp.max(m8, axis=0, keepdims=True)
        l8 = acc = None
        for k0 in range(0, tk, VALUE_CHUNK):
            k1 = min(k0 + VALUE_CHUNK, tk)
            p = jnp.exp2(s[k0:k1, :] - m)
            cl = _rows8_reduce(p, jnp.add)
            l8 = cl if l8 is None else l8 + cl
            d = jnp.dot(vt[:, k0:k1], p.astype(BF16), preferred_element_type=F32)
            acc = d if acc is None else acc + d
        inv = 1.0 / jnp.sum(l8, axis=0, keepdims=True)
        odt = acc[:, :tq] * inv[:, :tq] - acc[:, tq:] * (lam * inv[:, tq:])
        o = _rms_scale(odt.T) * out_gain
        o_ref[i * tq:tk, :] = (o * ga_ref[i * tq:tk, :].astype(F32)).astype(BF16)

    nxt = scores(0)
    for i in range(n_tiles):
        cur = nxt
        if i + 1 < n_tiles:
            nxt = scores(i + 1)
        values(i, *cur)


def _attention(q, k, v, ga, lq1, lk1, lq2, lk2, sg, lam_init):
    B, _, S, _ = q.shape
    head_spec = pl.BlockSpec((None, None, S, V_HEAD_DIM), lambda b, h: (b, h, 0, 0))
    lam_spec = pl.BlockSpec((1, QK_HEAD_DIM), lambda b, h: (0, 0))
    return pl.pallas_call(
        functools.partial(_attn_kernel, lam_init=lam_init),
        grid=(B, ATTN_HEADS),
        in_specs=[lam_spec, lam_spec, lam_spec, lam_spec,
                  pl.BlockSpec((1, V_HEAD_DIM), lambda b, h: (0, 0)),
                  head_spec, head_spec, head_spec, head_spec],
        out_specs=head_spec,
        out_shape=jax.ShapeDtypeStruct((B, ATTN_HEADS, S, V_HEAD_DIM), BF16),
        compiler_params=pltpu.CompilerParams(
            dimension_semantics=("arbitrary", "arbitrary"), vmem_limit_bytes=VMEM_LIMIT_BYTES),
        name="diff_attn",
    )(lq1, lk1, lq2, lk2, sg, q, k, v, ga)


def _lru_kernel(xl_ref, gl_ref, cw_ref, cb_ref, wg_ref, br_ref, bi_ref, lam_ref, y_ref,
                xprev_ref, hprev_ref, a_ref, b_ref, *, batch):
    R = xl_ref.shape[0]
    halo = (CONV_WIDTH - 1) * batch

    @pl.when(pl.program_id(0) == 0)
    def _():
        xprev_ref[...] = jnp.zeros_like(xprev_ref)
        hprev_ref[...] = jnp.zeros_like(hprev_ref)

    x = xl_ref[...].astype(F32)
    xe = jnp.concatenate([xprev_ref[...], x], axis=0)
    xprev_ref[...] = x[R - halo:, :]
    xc = cb_ref[...] + sum(cw_ref[j:j + 1, :] * xe[j * batch:j * batch + R, :]
                           for j in range(CONV_WIDTH))

    pre = [jnp.dot(xc[:, n * LRU_BLOCK:(n + 1) * LRU_BLOCK].astype(BF16), wg_ref[n],
                   preferred_element_type=F32) for n in range(LRU_BLOCKS)]
    r = jax.nn.sigmoid(jnp.concatenate([g[:, :LRU_BLOCK] for g in pre], axis=1) + br_ref[...])
    gi = jax.nn.sigmoid(jnp.concatenate([g[:, LRU_BLOCK:] for g in pre], axis=1) + bi_ref[...])

    nl = -lam_ref[...]
    softplus = jnp.maximum(nl, 0.0) + jnp.log1p(jnp.exp(-jnp.abs(nl)))
    u = (LRU_C * softplus) * r
    a = jnp.exp(-u)
    z = jnp.tanh(u) * (a * a + 1.0)
    mult = jnp.where(z > 0.0, z * lax.rsqrt(z), 0.0)
    a_ref[...] = a
    b_ref[...] = mult * gi * xc

    h = hprev_ref[...]
    for t in range(R // batch):
        rows = slice(t * batch, (t + 1) * batch)
        h = a_ref[rows, :] * h + b_ref[rows, :]
        a_ref[rows, :] = h
    hprev_ref[...] = h
    y_ref[...] = (a_ref[...] * gl_ref[...].astype(F32)).astype(BF16)


def _lru(xl, gl, cw, cb, wg, br, bi, lam, batch):
    N, W = xl.shape
    R = LRU_ROWS
    row_spec = pl.BlockSpec((R, W), lambda i: (i, 0))
    vec_spec = pl.BlockSpec((1, W), lambda i: (0, 0))
    return pl.pallas_call(
        functools.partial(_lru_kernel, batch=batch),
        grid=(N // R,),
        in_specs=[row_spec, row_spec,
                  pl.BlockSpec((CONV_WIDTH, W), lambda i: (0, 0)), vec_spec,
                  pl.BlockSpec((LRU_BLOCKS, LRU_BLOCK, 2 * LRU_BLOCK), lambda i: (0, 0, 0)),
                  vec_spec, vec_spec, vec_spec],
        out_specs=row_spec,
        out_shape=jax.ShapeDtypeStruct((N, W), BF16),
        scratch_shapes=[pltpu.VMEM(((CONV_WIDTH - 1) * batch, W), F32),
                        pltpu.VMEM((batch, W), F32),
                        pltpu.VMEM((R, W), F32),
                        pltpu.VMEM((R, W), F32)],
        compiler_params=pltpu.CompilerParams(
            dimension_semantics=("arbitrary",), vmem_limit_bytes=VMEM_LIMIT_BYTES),
        name="rg_lru",
    )(xl, gl, cw, cb, wg, br, bi, lam)


def _outproj_kernel(ya_ref, yl_ref, w_ref, perm_t_ref, x_ref, g_ref, o_ref):
    B, H, ts, hd = ya_ref.shape
    ya = jnp.concatenate([ya_ref[:, h, :, :].reshape(B * ts, hd) for h in range(H)], axis=1)
    yl = jnp.dot(perm_t_ref[...], yl_ref[...], preferred_element_type=F32).astype(BF16)
    acc = jnp.dot(ya, w_ref[:ATTN_WIDTH, :], preferred_element_type=F32)
    acc = acc + jnp.dot(yl, w_ref[ATTN_WIDTH:, :], preferred_element_type=F32)
    y = _rms_scale(acc) * g_ref[...]
    o_ref[...] = x_ref[...] + y.reshape(B, ts, y.shape[-1])


def _outproj(ya, yl, w, perm_t, x, g):
    B, S, D = x.shape
    ts = ROW_TILE // B
    x_spec = pl.BlockSpec((B, ts, D), lambda i: (0, i, 0))
    return pl.pallas_call(
        _outproj_kernel,
        grid=(S // ts,),
        in_specs=[pl.BlockSpec((B, ATTN_HEADS, ts, V_HEAD_DIM), lambda i: (0, 0, i, 0)),
                  pl.BlockSpec((ROW_TILE, LRU_WIDTH), lambda i: (i, 0)),
                  pl.BlockSpec((ATTN_WIDTH + LRU_WIDTH, D), lambda i: (0, 0)),
                  pl.BlockSpec((ROW_TILE, ROW_TILE), lambda i: (0, 0)),
                  x_spec,
                  pl.BlockSpec((1, D), lambda i: (0, 0))],
        out_specs=x_spec,
        out_shape=jax.ShapeDtypeStruct((B, S, D), F32),
        compiler_params=pltpu.CompilerParams(
            dimension_semantics=("arbitrary",), vmem_limit_bytes=VMEM_LIMIT_BYTES),
        name="outproj",
    )(ya, yl, w, perm_t, x, g)


def kernel(x, pre_norm_g, w_in, lambda_q1, lambda_k1, lambda_q2, lambda_k2, subln_g, conv_w, conv_b,
           w_rgate, b_rgate, w_igate, b_igate, lru_lambda, w_out, post_norm_g):
    B, S, D = x.shape
    assert D == D_MODEL and S % ROW_TILE == 0 and S % Q_TILE == 0
    assert ROW_TILE % B == 0 and LRU_ROWS % B == 0 and (S * B) % LRU_ROWS == 0
    assert (CONV_WIDTH - 1) * B <= LRU_ROWS
    w_in_bf = w_in.astype(BF16)
    w_out_bf = w_out.astype(BF16)
    w_gate_bf = jnp.concatenate([w_rgate, w_igate], axis=-1).astype(BF16)
    ts = ROW_TILE // B
    r = np.arange(ROW_TILE)
    perm_np = np.zeros((ROW_TILE, ROW_TILE), np.float32)
    perm_np[r, (r % B) * ts + r // B] = 1.0
    perm = jnp.asarray(perm_np, BF16)
    perm_t = jnp.asarray(perm_np.T, BF16)
    for l in range(DEPTH):
        lam_init = 0.8 - 0.6 * math.exp(-0.3 * l)
        q, k, v, ga, xl, gl = _inproj(x, pre_norm_g[l][None], w_in_bf[l], perm)
        ya = _attention(q, k, v, ga, lambda_q1[l][None], lambda_k1[l][None], lambda_q2[l][None],
                        lambda_k2[l][None], subln_g[l][None], lam_init)
        yl = _lru(xl, gl, conv_w[l], conv_b[l][None], w_gate_bf[l], b_rgate[l][None],
                  b_igate[l][None], lru_lambda[l][None], B)
        x = _outproj(ya, yl, w_out_bf[l], perm_t, x, post_norm_g[l][None])
    return x
```
